```python
import jax, jax.numpy as jnp
from jax import lax
import numpy as np

D_MODEL = 2048
BATCH = 4
SEQ = 4096
DEPTH = 2

A_WIDTH = D_MODEL // 2
B_WIDTH = D_MODEL // 2
A_HEADS = 8
B_HEADS = 8
SHORT_CONV = 3
CONF_CONV = 31
IN_COLS = 3 * A_WIDTH + 2 * B_WIDTH
CHUNK = 128
C_WIDTH = 2 * D_MODEL
C_HEADS = 16
N_GROUPS = 8
EXPERTS_PER_GROUP = 8
N_EXPERTS = N_GROUPS * EXPERTS_PER_GROUP
TOP_K = 2
D_EXPERT = 512
ROW_BLOCK = 128
EPS = 1e-6

kernel_name = "hybrid_conv_gmlp_hmoe_block"


def rmsnorm(x, g):
    xf = x.astype(jnp.float32)
    y = xf * lax.rsqrt(jnp.mean(xf * xf, axis=-1, keepdims=True) + EPS)
    return (y * g.astype(jnp.float32)).astype(x.dtype)


def layernorm(x, g, b):
    xf = x.astype(jnp.float32)
    mu = jnp.mean(xf, axis=-1, keepdims=True)
    var = jnp.mean(jnp.square(xf - mu), axis=-1, keepdims=True)
    y = (xf - mu) * lax.rsqrt(var + EPS)
    return (y * g.astype(jnp.float32) + b.astype(jnp.float32)).astype(x.dtype)


def causal_dwconv(x, w):
    k, ch = w.shape
    return lax.conv_general_dilated(
        x, w[:, None, :].astype(x.dtype), window_strides=(1,),
        padding=((k - 1, 0),), dimension_numbers=("NWC", "WIO", "NWC"),
        feature_group_count=ch)


def conv_mixer(h, w_in, short_w, conf_w, conf_b, ln_g, ln_b, w_out):
    z = h @ w_in
    a_b, a_c, a_x, glu_a, glu_g = jnp.split(
        z, np.cumsum([A_WIDTH, A_WIDTH, A_WIDTH, B_WIDTH]).tolist(), axis=-1)
    y_a = a_b * causal_dwconv(a_c * a_x, short_w)
    g = glu_a * jax.nn.sigmoid(glu_g)
    g = causal_dwconv(g, conf_w) + conf_b
    y_b = jax.nn.silu(layernorm(g, ln_g, ln_b))
    return jnp.concatenate([y_a, y_b], axis=-1) @ w_out


def gmlp_mixer(h, w1, b1, ln_g, ln_b, ws, bs, w2):
    bsz, s, _ = h.shape
    z = jax.nn.gelu(h @ w1 + b1)
    u, v = jnp.split(z, 2, axis=-1)
    v = layernorm(v, ln_g, ln_b)
    vh = v.reshape(bsz, s // CHUNK, CHUNK, C_HEADS, C_WIDTH // C_HEADS)
    causal = jnp.tril(jnp.ones((CHUNK, CHUNK), dtype=bool))
    wm = jnp.where(causal[None], ws, jnp.zeros((), ws.dtype))
    vm = jnp.einsum("hij,bnjhd->bnihd", wm, vh) + bs.T[None, None, :, :, None]
    return (u * vm.reshape(bsz, s, C_WIDTH)) @ w2


def hier_moe(h, gw, gb, ew, eb, w1, w3, w2):
    bsz, s, d = h.shape
    t = bsz * s
    xt = h.reshape(t, d)
    g_logits = (xt @ gw + gb).astype(jnp.float32)
    g_prob = jax.nn.softmax(g_logits, axis=-1)
    g_idx = jnp.argmax(g_logits, axis=-1)
    g_w = jnp.take_along_axis(g_prob, g_idx[:, None], axis=1)[:, 0]
    e_logits = (xt @ ew + eb).astype(jnp.float32).reshape(t, N_GROUPS, EXPERTS_PER_GROUP)
    e_sel = jnp.take_along_axis(e_logits, g_idx[:, None, None], axis=1)[:, 0]
    top_v, top_i = lax.top_k(e_sel, TOP_K)
    comb = g_w[:, None] * jax.nn.softmax(top_v, axis=-1)
    expert = g_idx[:, None].astype(jnp.int32) * EXPERTS_PER_GROUP + top_i.astype(jnp.int32)
    m = t * TOP_K
    flat_e = expert.reshape(m)
    flat_tok = jnp.repeat(jnp.arange(t, dtype=jnp.int32), TOP_K)
    flat_w = comb.reshape(m)
    order = jnp.argsort(flat_e)
    se, stok, sw = flat_e[order], flat_tok[order], flat_w[order]
    counts = jnp.zeros((N_EXPERTS,), jnp.int32).at[flat_e].add(1)
    starts = jnp.cumsum(counts) - counts
    pcounts = (counts + ROW_BLOCK - 1) // ROW_BLOCK * ROW_BLOCK
    pends = jnp.cumsum(pcounts)
    pstarts = pends - pcounts
    dest = pstarts[se] + (jnp.arange(m, dtype=jnp.int32) - starts[se])
    n_blocks = -(-m // ROW_BLOCK) + N_EXPERTS
    rows = n_blocks * ROW_BLOCK
    row_tok = jnp.full((rows,), t, jnp.int32).at[dest].set(stok)
    row_w = jnp.zeros((rows,), h.dtype).at[dest].set(sw.astype(h.dtype))
    block_e = jnp.minimum(
        jnp.searchsorted(pends, jnp.arange(n_blocks, dtype=jnp.int32) * ROW_BLOCK, side="right"),
        N_EXPERTS - 1)
    x_pad = jnp.concatenate([xt, jnp.zeros((1, d), xt.dtype)], axis=0)
    xb = x_pad[row_tok].reshape(n_blocks, ROW_BLOCK, d)

    def expert_block(args):
        xblk, e = args
        return (jax.nn.silu(xblk @ w1[e]) * (xblk @ w3[e])) @ w2[e]

    yb = lax.map(expert_block, (xb, block_e)).reshape(rows, d)
    out = jnp.zeros((t + 1, d), h.dtype).at[row_tok].add(yb * row_w[:, None])[:t]
    return out.reshape(bsz, s, d)


def setup_inputs(seed: int = 0) -> dict:
    key = jax.random.key(seed)
    ks = jax.random.split(key, 40)
    ne = (DEPTH + 1) // 2
    no = DEPTH // 2
    f32 = jnp.float32

    def nrm(k, shape, s):
        return jax.random.normal(k, shape, f32) * s

    def gain(k, shape):
        return 1.0 + nrm(k, shape, 0.02)

    d = D_MODEL
    return {
        "x": nrm(ks[0], (BATCH, SEQ, d), 1.0),
        "c": nrm(ks[1], (BATCH, d), 1.0),
        "ada_w": nrm(ks[2], (DEPTH, d, 6 * d), 0.5 * d ** -0.5),
        "ada_b": nrm(ks[3], (DEPTH, 6 * d), 0.02),
        "mix_norm_g": gain(ks[4], (DEPTH, d)),
        "ffn_norm_g": gain(ks[5], (DEPTH, d)),
        "cv_w_in": nrm(ks[6], (ne, d, IN_COLS), d ** -0.5),
        "cv_short_w": nrm(ks[7], (ne, SHORT_CONV, A_WIDTH), SHORT_CONV ** -0.5),
        "cv_conf_w": nrm(ks[8], (ne, CONF_CONV, B_WIDTH), CONF_CONV ** -0.5),
        "cv_conf_b": nrm(ks[9], (ne, B_WIDTH), 0.02),
        "cv_ln_g": gain(ks[10], (ne, B_WIDTH)),
        "cv_ln_b": nrm(ks[11], (ne, B_WIDTH), 0.02),
        "cv_w_out": nrm(ks[12], (ne, A_WIDTH + B_WIDTH, d), (A_WIDTH + B_WIDTH) ** -0.5),
        "sg_w1": nrm(ks[13], (no, d, 2 * C_WIDTH), d ** -0.5),
        "sg_b1": nrm(ks[14], (no, 2 * C_WIDTH), 0.02),
        "sg_ln_g": gain(ks[15], (no, C_WIDTH)),
        "sg_ln_b": nrm(ks[16], (no, C_WIDTH), 0.02),
        "sg_ws": nrm(ks[17], (no, C_HEADS, CHUNK, CHUNK), CHUNK ** -0.5),
        "sg_bs": gain(ks[18], (no, C_HEADS, CHUNK)),
        "sg_w2": nrm(ks[19], (no, C_WIDTH, d), C_WIDTH ** -0.5),
        "rt_gw": nrm(ks[20], (DEPTH, d, N_GROUPS), d ** -0.5),
        "rt_gb": nrm(ks[21], (DEPTH, N_GROUPS), 0.01),
        "rt_ew": nrm(ks[22], (DEPTH, d, N_EXPERTS), d ** -0.5),
        "rt_eb": nrm(ks[23], (DEPTH, N_EXPERTS), 0.01),
        "ex_w1": nrm(ks[24], (DEPTH, N_EXPERTS, d, D_EXPERT), d ** -0.5),
        "ex_w3": nrm(ks[25], (DEPTH, N_EXPERTS, d, D_EXPERT), d ** -0.5),
        "ex_w2": nrm(ks[26], (DEPTH, N_EXPERTS, D_EXPERT, d), D_EXPERT ** -0.5),
        "final_g": gain(ks[27], (d,)),
    }


def reference(x, c, ada_w, ada_b, mix_norm_g, ffn_norm_g,
              cv_w_in, cv_short_w, cv_conf_w, cv_conf_b, cv_ln_g, cv_ln_b, cv_w_out,
              sg_w1, sg_b1, sg_ln_g, sg_ln_b, sg_ws, sg_bs, sg_w2,
              rt_gw, rt_gb, rt_ew, rt_eb, ex_w1, ex_w3, ex_w2, final_g):
    c_act = jax.nn.silu(c)
    for i in range(DEPTH):
        mod = c_act @ ada_w[i] + ada_b[i]
        sh1, sc1, g1, sh2, sc2, g2 = [m[:, None, :] for m in jnp.split(mod, 6, axis=-1)]
        h = rmsnorm(x, mix_norm_g[i]) * (1 + sc1) + sh1
        if i % 2 == 0:
            j = i // 2
            y = conv_mixer(h, cv_w_in[j], cv_short_w[j], cv_conf_w[j], cv_conf_b[j],
                           cv_ln_g[j], cv_ln_b[j], cv_w_out[j])
        else:
            j = i // 2
            y = gmlp_mixer(h, sg_w1[j], sg_b1[j], sg_ln_g[j], sg_ln_b[j],
                           sg_ws[j], sg_bs[j], sg_w2[j])
        x = x + g1 * y
        h = rmsnorm(x, ffn_norm_g[i]) * (1 + sc2) + sh2
        x = x + g2 * hier_moe(h, rt_gw[i], rt_gb[i], rt_ew[i], rt_eb[i],
                              ex_w1[i], ex_w3[i], ex_w2[i])
    return rmsnorm(x, final_g)
```

```python
import functools

import jax
import jax.numpy as jnp
from jax import lax
from jax.experimental import pallas as pl
from jax.experimental.pallas import tpu as pltpu

F32 = jnp.float32
BF16 = jnp.bfloat16
EPS = 1e-6
TOP_K = 2
LANES = 128
SUBLANES = 8
VMEM_LIMIT = 56 * 1024 * 1024
NEG = -1e30


def _pick(n, pref):
    if n <= pref:
        return n
    for t in range(pref - pref % LANES, 0, -LANES):
        if n % t == 0:
            return t
    raise ValueError(f"no lane-aligned tile for {n}")


def _params(sem, vmem=VMEM_LIMIT):
    return pltpu.CompilerParams(dimension_semantics=sem, vmem_limit_bytes=vmem)


def _dot(a, b):
    return jnp.dot(a, b, preferred_element_type=F32)


def _modnorm(x, g, sc, sh):
    ms = jnp.mean(x * x, axis=-1, keepdims=True)
    return (x * lax.rsqrt(ms + EPS) * g) * (1.0 + sc) + sh


def _ada_kernel(c_ref, w_ref, b_ref, o_ref):
    c = c_ref[...]
    ca = (c * jax.nn.sigmoid(c)).astype(BF16)
    o_ref[...] = _dot(ca, w_ref[...].astype(BF16)) + b_ref[...]


def _ada(c, ada_w, ada_b):
    L, D, N = ada_w.shape
    B = c.shape[0]
    bp = -(-B // SUBLANES) * SUBLANES
    cp = jnp.zeros((bp, D), F32).at[:B].set(c)
    tn = _pick(N, 1024)
    out = pl.pallas_call(
        _ada_kernel,
        grid=(L, N // tn),
        in_specs=[pl.BlockSpec((bp, D), lambda l, j: (0, 0)),
                  pl.BlockSpec((None, D, tn), lambda l, j: (l, 0, j)),
                  pl.BlockSpec((None, 1, tn), lambda l, j: (l, 0, j))],
        out_specs=pl.BlockSpec((None, bp, tn), lambda l, j: (l, 0, j)),
        out_shape=jax.ShapeDtypeStruct((L, bp, N), F32),
        compiler_params=_params(("arbitrary", "arbitrary")),
        name="ada_mod",
    )(cp, ada_w, ada_b.reshape(L, 1, N))
    return out[:, :B].reshape(L, B, 6, D)


def _conv_in_kernel(x_ref, g_ref, sc_ref, sh_ref, wb_ref, wc_ref, wx_ref, wa_ref, wg_ref,
                    ab_ref, p_ref, gl_ref, h_scr):
    @pl.when(pl.program_id(1) == 0)
    def _():
        h_scr[...] = _modnorm(x_ref[...], g_ref[...], sc_ref[...], sh_ref[...]).astype(BF16)

    h = h_scr[...]
    ab_ref[...] = _dot(h, wb_ref[...]).astype(BF16)
    p_ref[...] = (_dot(h, wc_ref[...]) * _dot(h, wx_ref[...])).astype(BF16)
    ga = _dot(h, wa_ref[...])
    gg = _dot(h, wg_ref[...])
    gl_ref[...] = (ga * jax.nn.sigmoid(gg)).astype(BF16)


def _conv_in(x2, g, sc, sh, w_in, A, S):
    T, D = x2.shape
    tm = _pick(S, 512)
    tn = _pick(A, 512)
    na = A // tn
    tps = S // tm
    rowmod = pl.BlockSpec((None, 1, D), lambda i, j: (i // tps, 0, 0))

    def wspec(s):
        return pl.BlockSpec((D, tn), lambda i, j: (0, s * na + j))

    out = jax.ShapeDtypeStruct((T, A), BF16)
    ospec = pl.BlockSpec((tm, tn), lambda i, j: (i, j))
    return pl.pallas_call(
        _conv_in_kernel,
        grid=(T // tm, na),
        in_specs=[pl.BlockSpec((tm, D), lambda i, j: (i, 0)),
                  pl.BlockSpec((1, D), lambda i, j: (0, 0)),
                  rowmod, rowmod] + [wspec(s) for s in range(5)],
        out_specs=[ospec, ospec, ospec],
        out_shape=[out, out, out],
        scratch_shapes=[pltpu.VMEM((tm, D), BF16)],
        compiler_params=_params(("arbitrary", "arbitrary")),
        name="conv_in",
    )(x2, g, sc, sh, w_in, w_in, w_in, w_in, w_in)


HALO_G = 32
HALO_P = 8


def _conv_out_kernel(ab_ref, p_ref, gl_ref, x_ref, g1_ref, sw_ref, cw_ref, cb_ref, lg_ref, lb_ref,
                     wo_ref, o_ref, eg_scr, ep_scr, cv_scr, ya_scr, *, tm, tps, k_short, k_conf,
                     rc, cc):
    i = pl.program_id(0)
    width = cv_scr.shape[1]

    @pl.when(i % tps == 0)
    def _():
        eg_scr[0:HALO_G, :] = jnp.zeros((HALO_G, width), F32)
        ep_scr[0:HALO_P, :] = jnp.zeros((HALO_P, width), F32)

    @pl.when(i % tps != 0)
    def _():
        eg_scr[0:HALO_G, :] = eg_scr[tm:tm + HALO_G, :]
        ep_scr[0:HALO_P, :] = ep_scr[tm:tm + HALO_P, :]

    eg_scr[HALO_G:HALO_G + tm, :] = gl_ref[...].astype(F32)
    ep_scr[HALO_P:HALO_P + tm, :] = p_ref[...].astype(F32)

    og = HALO_G - (k_conf - 1)
    op = HALO_P - (k_short - 1)
    for r0 in range(0, tm, rc):
        for c0 in range(0, width, cc):
            cs = slice(c0, c0 + cc)
            acc = jnp.broadcast_to(cb_ref[:, cs], (rc, cc))
            for k in range(k_conf):
                acc = acc + cw_ref[k:k + 1, cs] * eg_scr[r0 + og + k:r0 + og + k + rc, cs]
            cv_scr[r0:r0 + rc, cs] = acc
            acc = sw_ref[0:1, cs] * ep_scr[r0 + op:r0 + op + rc, cs]
            for k in range(1, k_short):
                acc = acc + sw_ref[k:k + 1, cs] * ep_scr[r0 + op + k:r0 + op + k + rc, cs]
            ya_scr[r0:r0 + rc, cs] = (ab_ref[r0:r0 + rc, cs].astype(F32) * acc).astype(BF16)

    cv = cv_scr[...]
    mu = jnp.mean(cv, axis=-1, keepdims=True)
    var = jnp.mean(jnp.square(cv - mu), axis=-1, keepdims=True)
    yb = (cv - mu) * lax.rsqrt(var + EPS) * lg_ref[...] + lb_ref[...]
    yb = (yb * jax.nn.sigmoid(yb)).astype(BF16)
    y = _dot(ya_scr[...], wo_ref[0:width, :]) + _dot(yb, wo_ref[width:2 * width, :])
    o_ref[...] = x_ref[...] + g1_ref[...] * y


def _conv_out(ab, p, gl, x2, g1, short_w, conf_w, conf_b, ln_g, ln_b, w_out, S):
    T, D = x2.shape
    A = ab.shape[1]
    k_short, k_conf = short_w.shape[0], conf_w.shape[0]
    assert k_conf - 1 <= HALO_G and k_short - 1 <= HALO_P
    tm = _pick(S, 256)
    tps = S // tm
    rc = min(tm, 64)
    cc = min(A, 256)
    full = lambda shape: pl.BlockSpec(shape, lambda i: (0,) * len(shape))
    row = lambda w: pl.BlockSpec((tm, w), lambda i: (i, 0))
    kern = functools.partial(_conv_out_kernel, tm=tm, tps=tps, k_short=k_short, k_conf=k_conf,
                             rc=rc, cc=cc)
    return pl.pallas_call(
        kern,
        grid=(T // tm,),
        in_specs=[row(A), row(A), row(A), row(D),
                  pl.BlockSpec((None, 1, D), lambda i: (i // tps, 0, 0)),
                  full((k_short, A)), full((k_conf, A)), full((1, A)), full((1, A)), full((1, A)),
                  full((2 * A, D))],
        out_specs=row(D),
        out_shape=jax.ShapeDtypeStruct((T, D), F32),
        scratch_shapes=[pltpu.VMEM((tm + HALO_G, A), F32), pltpu.VMEM((tm + HALO_P, A), F32),
                        pltpu.VMEM((tm, A), F32), pltpu.VMEM((tm, A), BF16)],
        compiler_params=_params(("arbitrary",)),
        name="conv_out",
    )(ab, p, gl, x2, g1, short_w, conf_w, conf_b.reshape(1, A), ln_g.reshape(1, A),
      ln_b.reshape(1, A), w_out)


def _gmlp_in_kernel(x_ref, g_ref, sc_ref, sh_ref, w_ref, b_ref, z_ref, st_ref, h_scr, s1_scr,
                    s2_scr, *, nu, cw):
    j = pl.program_id(1)

    @pl.when(j == 0)
    def _():
        h_scr[...] = _modnorm(x_ref[...], g_ref[...], sc_ref[...], sh_ref[...]).astype(BF16)
        s1_scr[...] = jnp.zeros_like(s1_scr)
        s2_scr[...] = jnp.zeros_like(s2_scr)

    z = jax.nn.gelu(_dot(h_scr[...], w_ref[...]) + b_ref[...])
    z_ref[...] = z.astype(BF16)

    @pl.when(j >= nu)
    def _():
        a1 = s1_scr[...]
        a2 = s2_scr[...]
        for c0 in range(0, z.shape[1], LANES):
            zc = z[:, c0:c0 + LANES]
            a1 = a1 + zc
            a2 = a2 + zc * zc
        s1_scr[...] = a1
        s2_scr[...] = a2

    @pl.when(j == pl.num_programs(1) - 1)
    def _():
        mu = jnp.sum(s1_scr[...], axis=-1, keepdims=True) / cw
        var = jnp.sum(s2_scr[...], axis=-1, keepdims=True) / cw - mu * mu
        rstd = lax.rsqrt(var + EPS)
        tm = mu.shape[0]
        st_ref[:, 0:LANES] = jnp.broadcast_to(mu, (tm, LANES))
        st_ref[:, LANES:2 * LANES] = jnp.broadcast_to(rstd, (tm, LANES))


def _gmlp_in(x2, g, sc, sh, w1, b1, S):
    T, D = x2.shape
    N = w1.shape[1]
    cw = N // 2
    tm = _pick(S, 512)
    tn = _pick(cw, 512)
    tps = S // tm
    rowmod = pl.BlockSpec((None, 1, D), lambda i, j: (i // tps, 0, 0))
    kern = functools.partial(_gmlp_in_kernel, nu=cw // tn, cw=float(cw))
    return pl.pallas_call(
        kern,
        grid=(T // tm, N // tn),
        in_specs=[pl.BlockSpec((tm, D), lambda i, j: (i, 0)),
                  pl.BlockSpec((1, D), lambda i, j: (0, 0)),
                  rowmod, rowmod,
                  pl.BlockSpec((D, tn), lambda i, j: (0, j)),
                  pl.BlockSpec((1, tn), lambda i, j: (0, j))],
        out_specs=[pl.BlockSpec((tm, tn), lambda i, j: (i, j)),
                   pl.BlockSpec((tm, 2 * LANES), lambda i, j: (i, 0))],
        out_shape=[jax.ShapeDtypeStruct((T, N), BF16),
                   jax.ShapeDtypeStruct((T, 2 * LANES), F32)],
        scratch_shapes=[pltpu.VMEM((tm, D), BF16), pltpu.VMEM((tm, LANES), F32),
                        pltpu.VMEM((tm, LANES), F32)],
        compiler_params=_params(("arbitrary", "arbitrary")),
        name="gmlp_in",
    )(x2, g, sc, sh, w1, b1.reshape(1, N))


def _gmlp_out_kernel(u_ref, v_ref, st_ref, lg_ref, lb_ref, ws_ref, bst_ref, w2_ref, x_ref, g1_ref,
                     o_ref, gt_scr, *, chunk, heads, hd):
    @pl.when(pl.program_id(1) == 0)
    def _():
        tm = u_ref.shape[0]
        ri = lax.broadcasted_iota(jnp.int32, (chunk, chunk), 0)
        ci = lax.broadcasted_iota(jnp.int32, (chunk, chunk), 1)
        causal = ri >= ci
        for r0 in range(0, tm, chunk):
            rows = slice(r0, r0 + chunk)
            mu = st_ref[rows, 0:1]
            rstd = st_ref[rows, LANES:LANES + 1]
            for h in range(heads):
                cols = slice(h * hd, (h + 1) * hd)
                vn = (v_ref[rows, cols].astype(F32) - mu) * rstd * lg_ref[:, cols] + lb_ref[:, cols]
                wm = jnp.where(causal, ws_ref[h], 0.0).astype(BF16)
                vm = _dot(wm, vn.astype(BF16)) + bst_ref[:, h:h + 1]
                gt_scr[rows, cols] = (u_ref[rows, cols].astype(F32) * vm).astype(BF16)

    o_ref[...] = x_ref[...] + g1_ref[...] * _dot(gt_scr[...], w2_ref[...])


def _gmlp_out(z, st, x2, g1, ln_g, ln_b, ws, bs, w2, S):
    T, D = x2.shape
    cw = w2.shape[0]
    heads, chunk, _ = ws.shape
    hd = cw // heads
    tm = _pick(S, 512)
    tn = _pick(D, 512)
    tps = S // tm
    nu = 1
    kern = functools.partial(_gmlp_out_kernel, chunk=chunk, heads=heads, hd=hd)
    full = lambda shape: pl.BlockSpec(shape, lambda i, j: (0,) * len(shape))
    return pl.pallas_call(
        kern,
        grid=(T // tm, D // tn),
        in_specs=[pl.BlockSpec((tm, cw), lambda i, j: (i, 0)),
                  pl.BlockSpec((tm, cw), lambda i, j: (i, nu)),
                  pl.BlockSpec((tm, 2 * LANES), lambda i, j: (i, 0)),
                  full((1, cw)), full((1, cw)), full((heads, chunk, chunk)), full((chunk, heads)),
                  pl.BlockSpec((cw, tn), lambda i, j: (0, j)),
                  pl.BlockSpec((tm, tn), lambda i, j: (i, j)),
                  pl.BlockSpec((None, 1, tn), lambda i, j: (i // tps, 0, j))],
        out_specs=pl.BlockSpec((tm, tn), lambda i, j: (i, j)),
        out_shape=jax.ShapeDtypeStruct((T, D), F32),
        scratch_shapes=[pltpu.VMEM((tm, cw), BF16)],
        compiler_params=_params(("arbitrary", "arbitrary")),
        name="gmlp_out",
    )(z, z, st, ln_g.reshape(1, cw), ln_b.reshape(1, cw), ws, bs.T, w2, x2, g1)


def _router_kernel(x_ref, g_ref, sc_ref, sh_ref, wr_ref, br_ref, lt_ref, h_ref, meta_ref, cnt_ref,
                   carry_scr, *, n_groups, epg, n_tiles):
    i = pl.program_id(0)
    ne = n_groups * epg
    tm = x_ref.shape[0]

    @pl.when(i == 0)
    def _():
        carry_scr[...] = jnp.zeros_like(carry_scr)

    @pl.when(i == n_tiles)
    def _():
        h_ref[...] = jnp.zeros_like(h_ref)

    @pl.when(i < n_tiles)
    def _():
        h = _modnorm(x_ref[...], g_ref[...], sc_ref[...], sh_ref[...])
        h_ref[...] = h
        logits = _dot(h.astype(BF16), wr_ref[...]) + br_ref[...]
        lane = lax.broadcasted_iota(jnp.int32, (tm, LANES), 1)
        big = jnp.int32(4 * LANES)

        def first_max(vals):
            m = jnp.max(vals, axis=-1, keepdims=True)
            idx = jnp.min(jnp.where(vals == m, lane, big), axis=-1, keepdims=True)
            return m, idx

        is_g = lane < n_groups
        gmax, gidx = first_max(jnp.where(is_g, logits, NEG))
        gsum = jnp.sum(jnp.where(is_g, jnp.exp(logits - gmax), 0.0), axis=-1, keepdims=True)
        g_w = 1.0 / gsum
        lo = n_groups + gidx * epg
        el = jnp.where((lane >= lo) & (lane < lo + epg), logits, NEG)
        m1, i1 = first_max(el)
        m2, i2 = first_max(jnp.where(lane == i1, NEG, el))
        e2 = jnp.exp(m2 - m1)
        den = 1.0 + e2
        c1 = g_w * (1.0 / den)
        c2 = g_w * (e2 / den)
        ex1 = i1 - n_groups
        ex2 = i2 - n_groups

        hit1 = lane == ex1
        hit2 = lane == ex2 + ne
        oh = jnp.where(hit1 | hit2, 1.0, 0.0)
        prefix = _dot(lt_ref[...], oh.astype(BF16))
        cs = jnp.sum(oh, axis=0, keepdims=True)
        cs_sw = pltpu.roll(cs, ne, axis=1)
        carry = carry_scr[...]
        base = prefix + carry + jnp.where(lane[0:1] >= ne, cs_sw, 0.0)
        r1 = jnp.sum(jnp.where(hit1, base, 0.0), axis=-1, keepdims=True)
        r2 = jnp.sum(jnp.where(hit2, base, 0.0), axis=-1, keepdims=True)
        carry_scr[...] = carry + cs + cs_sw

        meta = jnp.where(lane == 0, ex1.astype(F32), 0.0)
        meta = jnp.where(lane == 1, ex2.astype(F32), meta)
        meta = jnp.where(lane == 2, c1, meta)
        meta = jnp.where(lane == 3, c2, meta)
        meta = jnp.where(lane == 4, r1, meta)
        meta = jnp.where(lane == 5, r2, meta)
        meta_ref[...] = meta

    cnt_ref[...] = jnp.broadcast_to(carry_scr[...], cnt_ref.shape)


def _router(x2, g, sc, sh, gw, gb, ew, eb, S):
    T, D = x2.shape
    n_groups = gw.shape[1]
    ne = ew.shape[1]
    epg = ne // n_groups
    assert n_groups + ne <= LANES and TOP_K * ne == LANES
    tm = _pick(S, 512)
    tps = S // tm
    n_tiles = T // tm
    wr = jnp.zeros((D, LANES), F32).at[:, :n_groups].set(gw).at[:, n_groups:n_groups + ne].set(ew)
    br = jnp.zeros((1, LANES), F32).at[0, :n_groups].set(gb).at[0, n_groups:n_groups + ne].set(eb)
    lt = jnp.tril(jnp.ones((tm, tm), BF16), -1)
    last = n_tiles - 1
    rowmod = pl.BlockSpec((None, 1, D), lambda i: (jnp.minimum(i, last) // tps, 0, 0))
    kern = functools.partial(_router_kernel, n_groups=n_groups, epg=epg, n_tiles=n_tiles)
    return pl.pallas_call(
        kern,
        grid=(n_tiles + 1,),
        in_specs=[pl.BlockSpec((tm, D), lambda i: (jnp.minimum(i, last), 0)),
                  pl.BlockSpec((1, D), lambda i: (0, 0)),
                  rowmod, rowmod,
                  pl.BlockSpec((D, LANES), lambda i: (0, 0)),
                  pl.BlockSpec((1, LANES), lambda i: (0, 0)),
                  pl.BlockSpec((tm, tm), lambda i: (0, 0))],
        out_specs=[pl.BlockSpec((tm, D), lambda i: (i, 0)),
                   pl.BlockSpec((tm, LANES), lambda i: (jnp.minimum(i, last), 0)),
                   pl.BlockSpec((SUBLANES, LANES), lambda i: (0, 0))],
        out_shape=[jax.ShapeDtypeStruct((T + tm, D), F32),
                   jax.ShapeDtypeStruct((T, LANES), F32),
                   jax.ShapeDtypeStruct((SUBLANES, LANES), F32)],
        scratch_shapes=[pltpu.VMEM((1, LANES), F32)],
        compiler_params=_params(("arbitrary",)),
        name="router",
    )(x2, g, sc, sh, wr.astype(BF16), br, lt)


def _gather_kernel(tok_ref, nu_ref, h_hbm, o_ref, sem, *, blk):
    b = pl.program_id(0)

    @pl.when(b >= nu_ref[0])
    def _():
        o_ref[...] = jnp.zeros_like(o_ref)

    @pl.when(b < nu_ref[0])
    def _():
        def copy(r):
            return pltpu.make_async_copy(h_hbm.at[pl.ds(tok_ref[b * blk + r], 1)],
                                         o_ref.at[pl.ds(r, 1)], sem)

        def start(r, c):
            copy(r).start()
            return c

        def wait(r, c):
            copy(r).wait()
            return c

        lax.fori_loop(0, blk, start, 0)
        lax.fori_loop(0, blk, wait, 0)


def _gather(h, row_tok, n_used, blk, nb):
    D = h.shape[1]
    return pl.pallas_call(
        functools.partial(_gather_kernel, blk=blk),
        grid_spec=pltpu.PrefetchScalarGridSpec(
            num_scalar_prefetch=2,
            grid=(nb,),
            in_specs=[pl.BlockSpec(memory_space=pl.ANY)],
            out_specs=pl.BlockSpec((blk, D), lambda b, tok, nu: (b, 0)),
            scratch_shapes=[pltpu.SemaphoreType.DMA]),
        out_shape=jax.ShapeDtypeStruct((nb * blk, D), F32),
        compiler_params=_params(("arbitrary",)),
        name="moe_gather",
    )(row_tok, n_used, h)


def _expert_kernel(be_ref, nu_ref, xs_ref, w1_ref, w3_ref, w2_ref, y_ref, w1_scr, w3_scr, w2_scr):
    b = pl.program_id(0)

    @pl.when(b >= nu_ref[0])
    def _():
        y_ref[...] = jnp.zeros_like(y_ref)

    @pl.when(b < nu_ref[0])
    def _():
        prev = be_ref[jnp.maximum(b - 1, 0)]

        @pl.when((b == 0) | (be_ref[b] != prev))
        def _():
            w1_scr[...] = w1_ref[...].astype(BF16)
            w3_scr[...] = w3_ref[...].astype(BF16)
            w2_scr[...] = w2_ref[...].astype(BF16)

        x = xs_ref[...].astype(BF16)
        h1 = _dot(x, w1_scr[...])
        h3 = _dot(x, w3_scr[...])
        a = (h1 * jax.nn.sigmoid(h1)) * h3
        y_ref[...] = _dot(a.astype(BF16), w2_scr[...])


def _experts(xs, block_e, n_used, w1, w3, w2, blk, nb):
    D = xs.shape[1]
    Fe = w1.shape[2]
    rows = lambda b, be, nu: (jnp.minimum(b, nu[0] - 1), 0)
    wsel = lambda b, be, nu: (be[b], 0, 0)
    return pl.pallas_call(
        _expert_kernel,
        grid_spec=pltpu.PrefetchScalarGridSpec(
            num_scalar_prefetch=2,
            grid=(nb,),
            in_specs=[pl.BlockSpec((blk, D), rows),
                      pl.BlockSpec((None, D, Fe), wsel),
                      pl.BlockSpec((None, D, Fe), wsel),
                      pl.BlockSpec((None, Fe, D), wsel)],
            out_specs=pl.BlockSpec((blk, D), lambda b, be, nu: (b, 0)),
            scratch_shapes=[pltpu.VMEM((D, Fe), BF16), pltpu.VMEM((D, Fe), BF16),
                            pltpu.VMEM((Fe, D), BF16)]),
        out_shape=jax.ShapeDtypeStruct((nb * blk, D), F32),
        compiler_params=_params(("arbitrary",)),
        name="moe_experts",
    )(block_e, n_used, xs, w1, w3, w2)


def _combine_kernel(dest_ref, y_hbm, x_ref, meta_ref, g2_ref, fg_ref, o_ref, ya_scr, yb_scr, sem,
                    *, final):
    i = pl.program_id(0)
    tm = x_ref.shape[0]

    def copies(r):
        d0 = dest_ref[(i * tm + r) * TOP_K]
        d1 = dest_ref[(i * tm + r) * TOP_K + 1]
        return (pltpu.make_async_copy(y_hbm.at[pl.ds(d0, 1)], ya_scr.at[pl.ds(r, 1)], sem),
                pltpu.make_async_copy(y_hbm.at[pl.ds(d1, 1)], yb_scr.at[pl.ds(r, 1)], sem))

    def start(r, c):
        ca, cb = copies(r)
        ca.start()
        cb.start()
        return c

    def wait(r, c):
        ca, cb = copies(r)
        ca.wait()
        cb.wait()
        return c

    lax.fori_loop(0, tm, start, 0)
    lax.fori_loop(0, tm, wait, 0)
    c1 = meta_ref[:, 2:3]
    c2 = meta_ref[:, 3:4]
    xn = x_ref[...] + g2_ref[...] * (ya_scr[...] * c1 + yb_scr[...] * c2)
    if final:
        ms = jnp.mean(xn * xn, axis=-1, keepdims=True)
        xn = xn * lax.rsqrt(ms + EPS) * fg_ref[...]
    o_ref[...] = xn


def _combine(y, dest, x2, meta, g2, final_g, S, final):
    T, D = x2.shape
    tm = _pick(S, 256)
    tps = S // tm
    return pl.pallas_call(
        functools.partial(_combine_kernel, final=final),
        grid_spec=pltpu.PrefetchScalarGridSpec(
            num_scalar_prefetch=1,
            grid=(T // tm,),
            in_specs=[pl.BlockSpec(memory_space=pl.ANY),
                      pl.BlockSpec((tm, D), lambda i, d: (i, 0)),
                      pl.BlockSpec((tm, LANES), lambda i, d: (i, 0)),
                      pl.BlockSpec((None, 1, D), lambda i, d: (i // tps, 0, 0)),
                      pl.BlockSpec((1, D), lambda i, d: (0, 0))],
            out_specs=pl.BlockSpec((tm, D), lambda i, d: (i, 0)),
            scratch_shapes=[pltpu.VMEM((tm, D), F32), pltpu.VMEM((tm, D), F32),
                            pltpu.SemaphoreType.DMA]),
        out_shape=jax.ShapeDtypeStruct((T, D), F32),
        compiler_params=_params(("arbitrary",)),
        name="moe_combine",
    )(dest, y, x2, meta, g2, final_g)


def _moe(x2, g, sc, sh, g2, gw, gb, ew, eb, w1, w3, w2, final_g, S, final):
    T, D = x2.shape
    ne = ew.shape[1]
    blk = 256
    h, meta, cnt = _router(x2, g, sc, sh, gw, gb, ew, eb, S)
    counts = cnt[0, :ne].astype(jnp.int32)
    pcounts = (counts + blk - 1) // blk * blk
    pends = jnp.cumsum(pcounts)
    pstarts = pends - pcounts
    nb = (T * TOP_K) // blk + ne
    ex = meta[:, 0:TOP_K].astype(jnp.int32)
    rank = meta[:, 4:4 + TOP_K].astype(jnp.int32)
    dest = (pstarts[ex] + rank).reshape(T * TOP_K)
    tok = jnp.repeat(jnp.arange(T, dtype=jnp.int32), TOP_K)
    row_tok = jnp.full((nb * blk,), T, jnp.int32).at[dest].set(tok)
    block_e = jnp.minimum(
        jnp.searchsorted(pends, jnp.arange(nb, dtype=jnp.int32) * blk, side="right"),
        ne - 1).astype(jnp.int32)
    n_used = (pends[-1:] // blk).astype(jnp.int32)
    xs = _gather(h, row_tok, n_used, blk, nb)
    y = _experts(xs, block_e, n_used, w1, w3, w2, blk, nb)
    return _combine(y, dest, x2, meta, g2, final_g, S, final)


def kernel(x, c, ada_w, ada_b, mix_norm_g, ffn_norm_g, cv_w_in, cv_short_w, cv_conf_w, cv_conf_b,
           cv_ln_g, cv_ln_b, cv_w_out, sg_w1, sg_b1, sg_ln_g, sg_ln_b, sg_ws, sg_bs, sg_w2,
           rt_gw, rt_gb, rt_ew, rt_eb, ex_w1, ex_w3, ex_w2, final_g):
    B, S, D = x.shape
    depth = ada_w.shape[0]
    mod = _ada(c, ada_w, ada_b)
    x2 = x.reshape(B * S, D)
    fg = final_g.reshape(1, D)
    for i in range(depth):
        sh1, sc1, g1, sh2, sc2, g2 = [mod[i, :, k].reshape(B, 1, D) for k in range(6)]
        gm = mix_norm_g[i].reshape(1, D)
        j = i // 2
        if i % 2 == 0:
            A = cv_short_w.shape[2]
            assert cv_conf_w.shape[2] == A
            ab, p, gl = _conv_in(x2, gm, sc1, sh1, cv_w_in[j].astype(BF16), A, S)
            x2 = _conv_out(ab, p, gl, x2, g1, cv_short_w[j], cv_conf_w[j], cv_conf_b[j],
                           cv_ln_g[j], cv_ln_b[j], cv_w_out[j].astype(BF16), S)
        else:
            z, st = _gmlp_in(x2, gm, sc1, sh1, sg_w1[j].astype(BF16), sg_b1[j], S)
            x2 = _gmlp_out(z, st, x2, g1, sg_ln_g[j], sg_ln_b[j], sg_ws[j], sg_bs[j],
                           sg_w2[j].astype(BF16), S)
        x2 = _moe(x2, ffn_norm_g[i].reshape(1, D), sc2, sh2, g2, rt_gw[i], rt_gb[i], rt_ew[i],
                  rt_eb[i], ex_w1[i], ex_w3[i], ex_w2[i], fg, S, final=(i == depth - 1))
    return x2.reshape(B, S, D)
```
